```python
import jax, jax.numpy as jnp
from jax import lax
import numpy as np

D_MODEL = 1024
BATCH = 8
SEQ = 4096
DEPTH = 4

CHUNK = 64
N_MIXERS = 2
EXPAND = 2
BRANCH = EXPAND * D_MODEL
GMLP_BLOCK = 128
A_GROUPS = 8
A_GROUP_DIM = BRANCH // A_GROUPS
POOL_WINDOWS = (2, 4, 8, 16)
B_GROUPS = len(POOL_WINDOWS)
B_GROUP_DIM = BRANCH // B_GROUPS
N_A = (DEPTH + 1) // 2
N_B = DEPTH // 2
EPS = 1e-6

kernel_name = "hybrid_gmlp_pool_sandwich_trunk"


def rms_norm(x, g):
    xf = x.astype(jnp.float32)
    y = xf * lax.rsqrt(jnp.mean(xf * xf, axis=-1, keepdims=True) + EPS)
    return (y * g.astype(jnp.float32)).astype(x.dtype)


def layer_norm(x, g, b):
    xf = x.astype(jnp.float32)
    mu = jnp.mean(xf, axis=-1, keepdims=True)
    xc = xf - mu
    y = xc * lax.rsqrt(jnp.mean(xc * xc, axis=-1, keepdims=True) + EPS)
    return (y * g.astype(jnp.float32) + b.astype(jnp.float32)).astype(x.dtype)


def spatial_mask():
    p = jnp.arange(GMLP_BLOCK)
    return (p[None, :] // CHUNK) <= (p[:, None] // CHUNK)


def gmlp_mixer(h, w_in, ln_g, ln_b, w_s, b_s, w_out):
    B, S, _ = h.shape
    proj = h @ w_in
    u, v, z = jnp.split(proj, 3, axis=-1)
    u = jax.nn.gelu(u)
    v = layer_norm(jax.nn.gelu(v), ln_g, ln_b)
    vb = v.reshape(B, S // GMLP_BLOCK, GMLP_BLOCK, A_GROUPS, A_GROUP_DIM)
    w = jnp.where(spatial_mask()[None], w_s, jnp.zeros_like(w_s))
    mixed = jnp.einsum('gpq,bnqgc->bnpgc', w, vb)
    mixed = mixed + jnp.transpose(b_s)[None, None, :, :, None]
    mixed = mixed.reshape(B, S, BRANCH)
    y = u * mixed * jax.nn.silu(z)
    return y @ w_out


def pool_mixer(h, w_in, w_grp, scale, w_out):
    B, S, _ = h.shape
    proj = h @ w_in
    xb, z = jnp.split(proj, 2, axis=-1)
    xf = xb.astype(jnp.float32)
    cs = jnp.concatenate([jnp.zeros((B, 1, BRANCH), jnp.float32),
                          jnp.cumsum(xf, axis=1)], axis=1)
    upper = cs[:, 1:]
    t1 = jnp.arange(1, S + 1, dtype=jnp.int32)
    outs = []
    for gi, win in enumerate(POOL_WINDOWS):
        sl = slice(gi * B_GROUP_DIM, (gi + 1) * B_GROUP_DIM)
        lower = jnp.pad(cs[:, :S + 1 - win, sl], ((0, 0), (win - 1, 0), (0, 0)))
        count = jnp.minimum(t1, win).astype(jnp.float32)[None, :, None]
        pooled = (upper[:, :, sl] - lower) / count - xf[:, :, sl]
        outs.append(jnp.einsum('bsc,cd->bsd', pooled.astype(xb.dtype), w_grp[gi]))
    mixed = jnp.concatenate(outs, axis=-1) * scale
    y = mixed * jax.nn.silu(z)
    return y @ w_out


def setup_inputs(seed: int = 0) -> dict:
    key = jax.random.key(seed)
    ks = jax.random.split(key, 16)
    f32 = jnp.float32
    nrm = lambda k, shape, s: jax.random.normal(k, shape, f32) * s
    return {
        "x": nrm(ks[0], (BATCH, SEQ, D_MODEL), 1.0),
        "norm_pre": 1.0 + nrm(ks[1], (DEPTH, D_MODEL), 0.05),
        "norm_post": 1.0 + nrm(ks[2], (DEPTH, D_MODEL), 0.05),
        "a_w_in": nrm(ks[3], (N_A, D_MODEL, 3 * BRANCH), D_MODEL ** -0.5),
        "a_ln_g": 1.0 + nrm(ks[4], (N_A, BRANCH), 0.05),
        "a_ln_b": nrm(ks[5], (N_A, BRANCH), 0.02),
        "a_w_s": nrm(ks[6], (N_A, A_GROUPS, GMLP_BLOCK, GMLP_BLOCK), GMLP_BLOCK ** -0.5),
        "a_b_s": 1.0 + nrm(ks[7], (N_A, A_GROUPS, GMLP_BLOCK), 0.05),
        "a_w_out": nrm(ks[8], (N_A, BRANCH, D_MODEL), BRANCH ** -0.5),
        "b_w_in": nrm(ks[9], (N_B, D_MODEL, 2 * BRANCH), D_MODEL ** -0.5),
        "b_w_grp": nrm(ks[10], (N_B, B_GROUPS, B_GROUP_DIM, B_GROUP_DIM), B_GROUP_DIM ** -0.5),
        "b_scale": 1.0 + nrm(ks[11], (N_B, BRANCH), 0.1),
        "b_w_out": nrm(ks[12], (N_B, BRANCH, D_MODEL), BRANCH ** -0.5),
    }


def reference(x, norm_pre, norm_post, a_w_in, a_ln_g, a_ln_b, a_w_s, a_b_s, a_w_out,
              b_w_in, b_w_grp, b_scale, b_w_out):
    for i in range(DEPTH):
        h = rms_norm(x, norm_pre[i])
        j = i // N_MIXERS
        if i % N_MIXERS == 0:
            out = gmlp_mixer(h, a_w_in[j], a_ln_g[j], a_ln_b[j], a_w_s[j], a_b_s[j], a_w_out[j])
        else:
            out = pool_mixer(h, b_w_in[j], b_w_grp[j], b_scale[j], b_w_out[j])
        x = x + rms_norm(out, norm_post[i])
    return x
```

```python
import functools

import jax
import jax.numpy as jnp
from jax import lax
from jax.experimental import pallas as pl
from jax.experimental.pallas import tpu as pltpu

D_MODEL = 1024
BRANCH = 2048
CHUNK = 64
GMLP_BLOCK = 128
A_GROUPS = 8
A_GROUP_DIM = BRANCH // A_GROUPS
POOL_WINDOWS = (2, 4, 8, 16)
B_GROUPS = len(POOL_WINDOWS)
B_GROUP_DIM = BRANCH // B_GROUPS
HALO = max(POOL_WINDOWS)
EPS = 1e-6

ROW_TILE = 256
A_COL_CHUNK = 512
VMEM_LIMIT_BYTES = 56 * 1024 * 1024

_GELU_C = 0.7978845608028654


def _gelu_tanh(x):
    return 0.5 * x * (1.0 + jnp.tanh(_GELU_C * (x + 0.044715 * (x * x * x))))


def _silu(x):
    return x * jax.nn.sigmoid(x)


def _rms_scale(x):
    return lax.rsqrt(jnp.mean(x * x, axis=-1, keepdims=True) + EPS)


def _dot(a, b):
    return jnp.dot(a, b, preferred_element_type=jnp.float32)


def _gmlp_kernel(x_ref, gpre_ref, gpost_ref, win_ref, lng_ref, lnb_ref, ws_ref,
                 bias_ref, wout_ref, o_ref, h_s, uz_s, gv_s, y_s):
    tm = x_ref.shape[0]
    x = x_ref[...]
    h_s[...] = (x * _rms_scale(x) * gpre_ref[...]).astype(jnp.bfloat16)

    s1 = jnp.zeros((tm, 1), jnp.float32)
    s2 = jnp.zeros((tm, 1), jnp.float32)
    for j in range(BRANCH // A_COL_CHUNK):
        c0 = j * A_COL_CHUNK
        cols = slice(c0, c0 + A_COL_CHUNK)
        u = _dot(h_s[...], win_ref[:, c0:c0 + A_COL_CHUNK])
        z = _dot(h_s[...], win_ref[:, 2 * BRANCH + c0:2 * BRANCH + c0 + A_COL_CHUNK])
        uz_s[:, cols] = _gelu_tanh(u) * _silu(z)
        gv = _gelu_tanh(_dot(h_s[...], win_ref[:, BRANCH + c0:BRANCH + c0 + A_COL_CHUNK]))
        s1 = s1 + jnp.sum(gv, axis=-1, keepdims=True)
        s2 = s2 + jnp.sum(gv * gv, axis=-1, keepdims=True)
        gv_s[:, cols] = gv
    mu = s1 * (1.0 / BRANCH)
    var = s2 * (1.0 / BRANCH) - mu * mu
    rstd = lax.rsqrt(var + EPS)

    p_chunk = lax.broadcasted_iota(jnp.int32, (GMLP_BLOCK, GMLP_BLOCK), 0) // CHUNK
    q_chunk = lax.broadcasted_iota(jnp.int32, (GMLP_BLOCK, GMLP_BLOCK), 1) // CHUNK
    causal = q_chunk <= p_chunk

    for g in range(A_GROUPS):
        c0 = g * A_GROUP_DIM
        cols = slice(c0, c0 + A_GROUP_DIM)
        w_g = jnp.where(causal, ws_ref[g], 0.0).astype(jnp.bfloat16)
        vn = ((gv_s[:, cols] - mu) * rstd * lng_ref[:, cols] + lnb_ref[:, cols]).astype(jnp.bfloat16)
        for b in range(tm // GMLP_BLOCK):
            rows = slice(b * GMLP_BLOCK, (b + 1) * GMLP_BLOCK)
            mixed = _dot(w_g, vn[rows]) + bias_ref[:, cols]
            y_s[rows, cols] = (uz_s[rows, cols] * mixed).astype(jnp.bfloat16)

    out = _dot(y_s[...], wout_ref[...])
    o_ref[...] = x_ref[...] + out * _rms_scale(out) * gpost_ref[...]


def _gmlp_layer(x2d, g_pre, g_post, w_in, ln_g, ln_b, w_s, bias_full, w_out):
    t = x2d.shape[0]
    tm = ROW_TILE
    const = lambda i: (0, 0)
    resident = functools.partial(pl.BlockSpec, pipeline_mode=pl.Buffered(1))
    return pl.pallas_call(
        _gmlp_kernel,
        grid=(t // tm,),
        in_specs=[
            pl.BlockSpec((tm, D_MODEL), lambda i: (i, 0)),
            resident((1, D_MODEL), const),
            resident((1, D_MODEL), const),
            resident((D_MODEL, 3 * BRANCH), const),
            resident((1, BRANCH), const),
            resident((1, BRANCH), const),
            resident((A_GROUPS, GMLP_BLOCK, GMLP_BLOCK), lambda i: (0, 0, 0)),
            resident((GMLP_BLOCK, BRANCH), const),
            resident((BRANCH, D_MODEL), const),
        ],
        out_specs=pl.BlockSpec((tm, D_MODEL), lambda i: (i, 0)),
        out_shape=jax.ShapeDtypeStruct(x2d.shape, x2d.dtype),
        scratch_shapes=[
            pltpu.VMEM((tm, D_MODEL), jnp.bfloat16),
            pltpu.VMEM((tm, BRANCH), jnp.float32),
            pltpu.VMEM((tm, BRANCH), jnp.float32),
            pltpu.VMEM((tm, BRANCH), jnp.bfloat16),
        ],
        compiler_params=pltpu.CompilerParams(
            dimension_semantics=("arbitrary",),
            vmem_limit_bytes=VMEM_LIMIT_BYTES),
        name="gmlp_layer",
    )(x2d, g_pre, g_post, w_in, ln_g, ln_b, w_s, bias_full, w_out)


def _pool_kernel(x_ref, gpre_ref, gpost_ref, win_ref, wgrp_ref, scale_ref,
                 wout_ref, o_ref, h_s, carry_s, y_s):
    tm = x_ref.shape[0]
    sidx = pl.program_id(1)

    @pl.when(sidx == 0)
    def _():
        carry_s[...] = jnp.zeros_like(carry_s)

    x = x_ref[...]
    h_s[...] = (x * _rms_scale(x) * gpre_ref[...]).astype(jnp.bfloat16)

    pos1 = sidx * tm + lax.broadcasted_iota(jnp.int32, (tm, 1), 0) + 1

    for g, win in enumerate(POOL_WINDOWS):
        c0 = g * B_GROUP_DIM
        cols = slice(c0, c0 + B_GROUP_DIM)
        xb = _dot(h_s[...], win_ref[:, c0:c0 + B_GROUP_DIM])
        z = _dot(h_s[...], win_ref[:, BRANCH + c0:BRANCH + c0 + B_GROUP_DIM])
        ext = jnp.concatenate([carry_s[:, cols], xb], axis=0)
        carry_s[:, cols] = xb[tm - HALO:, :]
        acc = ext
        k = 1
        while k < win:
            acc = acc + pltpu.roll(acc, k, 0)
            k *= 2
        wsum = acc[HALO:, :]
        inv_count = 1.0 / jnp.minimum(pos1, win).astype(jnp.float32)
        pooled = (wsum * inv_count - xb).astype(jnp.bfloat16)
        mixed = _dot(pooled, wgrp_ref[g]) * scale_ref[:, cols]
        y_s[:, cols] = (mixed * _silu(z)).astype(jnp.bfloat16)

    out = _dot(y_s[...], wout_ref[...])
    o_ref[...] = x_ref[...] + out * _rms_scale(out) * gpost_ref[...]


def _pool_layer(x3d, g_pre, g_post, w_in, w_grp, scale, w_out):
    b, s, _ = x3d.shape
    tm = ROW_TILE
    const = lambda i, j: (0, 0)
    resident = functools.partial(pl.BlockSpec, pipeline_mode=pl.Buffered(1))
    return pl.pallas_call(
        _pool_kernel,
        grid=(b, s // tm),
        in_specs=[
            pl.BlockSpec((None, tm, D_MODEL), lambda i, j: (i, j, 0)),
            resident((1, D_MODEL), const),
            resident((1, D_MODEL), const),
            resident((D_MODEL, 2 * BRANCH), const),
            resident((B_GROUPS, B_GROUP_DIM, B_GROUP_DIM), lambda i, j: (0, 0, 0)),
            resident((1, BRANCH), const),
            resident((BRANCH, D_MODEL), const),
        ],
        out_specs=pl.BlockSpec((None, tm, D_MODEL), lambda i, j: (i, j, 0)),
        out_shape=jax.ShapeDtypeStruct(x3d.shape, x3d.dtype),
        scratch_shapes=[
            pltpu.VMEM((tm, D_MODEL), jnp.bfloat16),
            pltpu.VMEM((HALO, BRANCH), jnp.float32),
            pltpu.VMEM((tm, BRANCH), jnp.bfloat16),
        ],
        compiler_params=pltpu.CompilerParams(
            dimension_semantics=("arbitrary", "arbitrary"),
            vmem_limit_bytes=VMEM_LIMIT_BYTES),
        name="pool_layer",
    )(x3d, g_pre, g_post, w_in, w_grp, scale, w_out)


def kernel(x, norm_pre, norm_post, a_w_in, a_ln_g, a_ln_b, a_w_s, a_b_s, a_w_out,
           b_w_in, b_w_grp, b_scale, b_w_out):
    bsz, seq, d = x.shape
    assert d == D_MODEL and seq % ROW_TILE == 0 and ROW_TILE % GMLP_BLOCK == 0
    depth = norm_pre.shape[0]
    bf16 = jnp.bfloat16
    row = lambda v: v.reshape(1, -1)
    for i in range(depth):
        j = i // 2
        if i % 2 == 0:
            bias_full = jnp.repeat(jnp.transpose(a_b_s[j]), A_GROUP_DIM, axis=1)
            x = _gmlp_layer(
                x.reshape(bsz * seq, d), row(norm_pre[i]), row(norm_post[i]),
                a_w_in[j].astype(bf16), row(a_ln_g[j]), row(a_ln_b[j]), a_w_s[j],
                bias_full, a_w_out[j].astype(bf16)).reshape(bsz, seq, d)
        else:
            x = _pool_layer(
                x, row(norm_pre[i]), row(norm_post[i]), b_w_in[j].astype(bf16),
                b_w_grp[j].astype(bf16), row(b_scale[j]), b_w_out[j].astype(bf16))
    return x
```

```python
import functools

import jax
import jax.numpy as jnp
from jax import lax
from jax.experimental import pallas as pl
from jax.experimental.pallas import tpu as pltpu

D_MODEL = 1024
BRANCH = 2048
CHUNK = 64
GMLP_BLOCK = 128
A_GROUPS = 8
A_GROUP_DIM = BRANCH // A_GROUPS
POOL_WINDOWS = (2, 4, 8, 16)
B_GROUPS = len(POOL_WINDOWS)
B_GROUP_DIM = BRANCH // B_GROUPS
HALO = max(POOL_WINDOWS)
EPS = 1e-6

ROW_TILE = 256
VMEM_LIMIT_BYTES = 56 * 1024 * 1024

_GELU_C = 0.7978845608028654


def _gelu_tanh(x):
    return 0.5 * x * (1.0 + jnp.tanh(_GELU_C * (x + 0.044715 * (x * x * x))))


def _silu(x):
    return x * jax.nn.sigmoid(x)


def _rms_scale(x):
    return lax.rsqrt(jnp.mean(x * x, axis=-1, keepdims=True) + EPS)


def _dot(a, b):
    return jnp.dot(a, b, preferred_element_type=jnp.float32)


def _pack_rows(w):
    k, n = w.shape[-2:]
    wb = w.astype(jnp.bfloat16).reshape(w.shape[:-2] + (k // 2, 2, n))
    return lax.bitcast_convert_type(jnp.swapaxes(wb, -1, -2), jnp.uint32)


def _unpack_rows(w32):
    return pltpu.bitcast(w32, jnp.bfloat16)


def _gmlp_kernel(x_ref, gpre_ref, gpost_ref, win_ref, lng_ref, lnb_ref, ws_ref,
                 bias_ref, wout_ref, o_ref, h_s, uz_s, gv_s, y_s):
    tm = x_ref.shape[0]
    x = x_ref[...]
    h_s[...] = (x * _rms_scale(x) * gpre_ref[...]).astype(jnp.bfloat16)

    gv = _gelu_tanh(_dot(h_s[...], _unpack_rows(win_ref[:, BRANCH:2 * BRANCH])))
    mu = jnp.mean(gv, axis=-1, keepdims=True)
    var = jnp.mean(gv * gv, axis=-1, keepdims=True) - mu * mu
    rstd = lax.rsqrt(var + EPS)
    gv_s[...] = gv

    u = _dot(h_s[...], _unpack_rows(win_ref[:, 0:BRANCH]))
    z = _dot(h_s[...], _unpack_rows(win_ref[:, 2 * BRANCH:3 * BRANCH]))
    uz_s[...] = _gelu_tanh(u) * _silu(z)

    p_chunk = lax.broadcasted_iota(jnp.int32, (GMLP_BLOCK, GMLP_BLOCK), 0) // CHUNK
    q_chunk = lax.broadcasted_iota(jnp.int32, (GMLP_BLOCK, GMLP_BLOCK), 1) // CHUNK
    causal = q_chunk <= p_chunk

    for g in range(A_GROUPS):
        c0 = g * A_GROUP_DIM
        cols = slice(c0, c0 + A_GROUP_DIM)
        w_g = jnp.where(causal, ws_ref[g], 0.0).astype(jnp.bfloat16)
        vn = ((gv_s[:, cols] - mu) * rstd * lng_ref[:, cols] + lnb_ref[:, cols]).astype(jnp.bfloat16)
        for b in range(tm // GMLP_BLOCK):
            rows = slice(b * GMLP_BLOCK, (b + 1) * GMLP_BLOCK)
            mixed = _dot(w_g, vn[rows]) + bias_ref[:, cols]
            y_s[rows, cols] = (uz_s[rows, cols] * mixed).astype(jnp.bfloat16)

    out = _dot(y_s[...], _unpack_rows(wout_ref[...]))
    o_ref[...] = x_ref[...] + out * _rms_scale(out) * gpost_ref[...]


def _gmlp_layer(x2d, g_pre, g_post, w_in, ln_g, ln_b, w_s, bias_full, w_out):
    t = x2d.shape[0]
    tm = ROW_TILE
    const = lambda i: (0, 0)
    resident = functools.partial(pl.BlockSpec, pipeline_mode=pl.Buffered(1))
    return pl.pallas_call(
        _gmlp_kernel,
        grid=(t // tm,),
        in_specs=[
            pl.BlockSpec((tm, D_MODEL), lambda i: (i, 0)),
            resident((1, D_MODEL), const),
            resident((1, D_MODEL), const),
            resident((D_MODEL // 2, 3 * BRANCH), const),
            resident((1, BRANCH), const),
            resident((1, BRANCH), const),
            resident((A_GROUPS, GMLP_BLOCK, GMLP_BLOCK), lambda i: (0, 0, 0)),
            resident((GMLP_BLOCK, BRANCH), const),
            resident((BRANCH // 2, D_MODEL), const),
        ],
        out_specs=pl.BlockSpec((tm, D_MODEL), lambda i: (i, 0)),
        out_shape=jax.ShapeDtypeStruct(x2d.shape, x2d.dtype),
        scratch_shapes=[
            pltpu.VMEM((tm, D_MODEL), jnp.bfloat16),
            pltpu.VMEM((tm, BRANCH), jnp.float32),
            pltpu.VMEM((tm, BRANCH), jnp.float32),
            pltpu.VMEM((tm, BRANCH), jnp.bfloat16),
        ],
        compiler_params=pltpu.CompilerParams(
            dimension_semantics=("arbitrary",),
            vmem_limit_bytes=VMEM_LIMIT_BYTES),
        name="gmlp_layer",
    )(x2d, g_pre, g_post, w_in, ln_g, ln_b, w_s, bias_full, w_out)


def _pool_kernel(x_ref, gpre_ref, gpost_ref, win_ref, wgrp_ref, scale_ref,
                 wout_ref, o_ref, h_s, carry_s, y_s):
    tm = x_ref.shape[0]
    sidx = pl.program_id(1)

    @pl.when(sidx == 0)
    def _():
        carry_s[...] = jnp.zeros_like(carry_s)

    x = x_ref[...]
    h_s[...] = (x * _rms_scale(x) * gpre_ref[...]).astype(jnp.bfloat16)

    pos1 = sidx * tm + lax.broadcasted_iota(jnp.int32, (tm, 1), 0) + 1

    for g, win in enumerate(POOL_WINDOWS):
        c0 = g * B_GROUP_DIM
        cols = slice(c0, c0 + B_GROUP_DIM)
        xb = _dot(h_s[...], _unpack_rows(win_ref[:, c0:c0 + B_GROUP_DIM]))
        z = _dot(h_s[...], _unpack_rows(win_ref[:, BRANCH + c0:BRANCH + c0 + B_GROUP_DIM]))
        ext = jnp.concatenate([carry_s[:, cols], xb], axis=0)
        carry_s[:, cols] = xb[tm - HALO:, :]
        acc = ext
        k = 1
        while k < win:
            acc = acc + pltpu.roll(acc, k, 0)
            k *= 2
        wsum = acc[HALO:, :]
        inv_count = 1.0 / jnp.minimum(pos1, win).astype(jnp.float32)
        pooled = (wsum * inv_count - xb).astype(jnp.bfloat16)
        mixed = _dot(pooled, _unpack_rows(wgrp_ref[g])) * scale_ref[:, cols]
        y_s[:, cols] = (mixed * _silu(z)).astype(jnp.bfloat16)

    out = _dot(y_s[...], _unpack_rows(wout_ref[...]))
    o_ref[...] = x_ref[...] + out * _rms_scale(out) * gpost_ref[...]


def _pool_layer(x3d, g_pre, g_post, w_in, w_grp, scale, w_out):
    b, s, _ = x3d.shape
    tm = ROW_TILE
    const = lambda i, j: (0, 0)
    resident = functools.partial(pl.BlockSpec, pipeline_mode=pl.Buffered(1))
    return pl.pallas_call(
        _pool_kernel,
        grid=(b, s // tm),
        in_specs=[
            pl.BlockSpec((None, tm, D_MODEL), lambda i, j: (i, j, 0)),
            resident((1, D_MODEL), const),
            resident((1, D_MODEL), const),
            resident((D_MODEL // 2, 2 * BRANCH), const),
            resident((B_GROUPS, B_GROUP_DIM // 2, B_GROUP_DIM), lambda i, j: (0, 0, 0)),
            resident((1, BRANCH), const),
            resident((BRANCH // 2, D_MODEL), const),
        ],
        out_specs=pl.BlockSpec((None, tm, D_MODEL), lambda i, j: (i, j, 0)),
        out_shape=jax.ShapeDtypeStruct(x3d.shape, x3d.dtype),
        scratch_shapes=[
            pltpu.VMEM((tm, D_MODEL), jnp.bfloat16),
            pltpu.VMEM((HALO, BRANCH), jnp.float32),
            pltpu.VMEM((tm, BRANCH), jnp.bfloat16),
        ],
        compiler_params=pltpu.CompilerParams(
            dimension_semantics=("arbitrary", "arbitrary"),
            vmem_limit_bytes=VMEM_LIMIT_BYTES),
        name="pool_layer",
    )(x3d, g_pre, g_post, w_in, w_grp, scale, w_out)


def kernel(x, norm_pre, norm_post, a_w_in, a_ln_g, a_ln_b, a_w_s, a_b_s, a_w_out,
           b_w_in, b_w_grp, b_scale, b_w_out):
    bsz, seq, d = x.shape
    assert d == D_MODEL and seq % ROW_TILE == 0 and ROW_TILE % GMLP_BLOCK == 0
    depth = norm_pre.shape[0]
    row = lambda v: v.reshape(1, -1)
    for i in range(depth):
        j = i // 2
        if i % 2 == 0:
            bias_full = jnp.repeat(jnp.transpose(a_b_s[j]), A_GROUP_DIM, axis=1)
            x = _gmlp_layer(
                x.reshape(bsz * seq, d), row(norm_pre[i]), row(norm_post[i]),
                _pack_rows(a_w_in[j]), row(a_ln_g[j]), row(a_ln_b[j]), a_w_s[j],
                bias_full, _pack_rows(a_w_out[j])).reshape(bsz, seq, d)
        else:
            x = _pool_layer(
                x, row(norm_pre[i]), row(norm_post[i]), _pack_rows(b_w_in[j]),
                _pack_rows(b_w_grp[j]), row(b_scale[j]), _pack_rows(b_w_out[j]))
    return x
```

```python
import functools

import jax
import jax.numpy as jnp
from jax import lax
from jax.experimental import pallas as pl
from jax.experimental.pallas import tpu as pltpu

D_MODEL = 1024
BRANCH = 2048
CHUNK = 64
GMLP_BLOCK = 128
A_GROUPS = 8
A_GROUP_DIM = BRANCH // A_GROUPS
POOL_WINDOWS = (2, 4, 8, 16)
B_GROUPS = len(POOL_WINDOWS)
B_GROUP_DIM = BRANCH // B_GROUPS
HALO = max(POOL_WINDOWS)
EPS = 1e-6

ROW_TILE = 256
WEIGHT_COPY_ROWS = 128
VMEM_LIMIT_BYTES = 56 * 1024 * 1024

_GELU_C = 0.7978845608028654


def _gelu_tanh(x):
    return 0.5 * x * (1.0 + jnp.tanh(_GELU_C * (x + 0.044715 * (x * x * x))))


def _silu(x):
    return x * jax.nn.sigmoid(x)


def _rms_scale(x):
    return lax.rsqrt(jnp.mean(x * x, axis=-1, keepdims=True) + EPS)


def _dot(a, b):
    return jnp.dot(a, b, preferred_element_type=jnp.float32)


def _copy_rows(src_ref, dst_ref):
    n = src_ref.shape[0] // WEIGHT_COPY_ROWS

    def body(i, carry):
        r = pl.multiple_of(i * WEIGHT_COPY_ROWS, WEIGHT_COPY_ROWS)
        dst_ref[pl.ds(r, WEIGHT_COPY_ROWS), :] = src_ref[pl.ds(r, WEIGHT_COPY_ROWS), :]
        return carry

    lax.fori_loop(0, n, body, 0)


def _gmlp_kernel(x_ref, gpre_ref, gpost_ref, win_ref, lng_ref, lnb_ref, ws_ref,
                 bias_ref, wout_ref, o_ref, h_s, uz_s, gv_s, y_s, win_s, wout_s):
    tm = x_ref.shape[0]

    @pl.when(pl.program_id(0) == 0)
    def _():
        _copy_rows(win_ref, win_s)
        _copy_rows(wout_ref, wout_s)

    x = x_ref[...]
    h_s[...] = (x * _rms_scale(x) * gpre_ref[...]).astype(jnp.bfloat16)

    gv = _gelu_tanh(_dot(h_s[...], win_s[:, BRANCH:2 * BRANCH]))
    mu = jnp.mean(gv, axis=-1, keepdims=True)
    var = jnp.mean(gv * gv, axis=-1, keepdims=True) - mu * mu
    rstd = lax.rsqrt(var + EPS)
    gv_s[...] = gv

    u = _dot(h_s[...], win_s[:, 0:BRANCH])
    z = _dot(h_s[...], win_s[:, 2 * BRANCH:3 * BRANCH])
    uz_s[...] = _gelu_tanh(u) * _silu(z)

    p_chunk = lax.broadcasted_iota(jnp.int32, (GMLP_BLOCK, GMLP_BLOCK), 0) // CHUNK
    q_chunk = lax.broadcasted_iota(jnp.int32, (GMLP_BLOCK, GMLP_BLOCK), 1) // CHUNK
    causal = q_chunk <= p_chunk

    for g in range(A_GROUPS):
        c0 = g * A_GROUP_DIM
        cols = slice(c0, c0 + A_GROUP_DIM)
        w_g = jnp.where(causal, ws_ref[g], 0.0).astype(jnp.bfloat16)
        vn = ((gv_s[:, cols] - mu) * rstd * lng_ref[:, cols] + lnb_ref[:, cols]).astype(jnp.bfloat16)
        for b in range(tm // GMLP_BLOCK):
            rows = slice(b * GMLP_BLOCK, (b + 1) * GMLP_BLOCK)
            mixed = _dot(w_g, vn[rows]) + bias_ref[:, cols]
            y_s[rows, cols] = (uz_s[rows, cols] * mixed).astype(jnp.bfloat16)

    out = _dot(y_s[...], wout_s[...])
    o_ref[...] = x_ref[...] + out * _rms_scale(out) * gpost_ref[...]


def _resident(block_shape, index_map):
    return pl.BlockSpec(block_shape, index_map, pipeline_mode=pl.Buffered(1))


def _layer_of(j, ndim):
    return lambda i: (j,) + (0,) * (ndim - 1)


_COMPILER_PARAMS = pltpu.CompilerParams(
    dimension_semantics=("arbitrary",),
    vmem_limit_bytes=VMEM_LIMIT_BYTES)


def _gmlp_layer(x2d, i, j, norm_pre, norm_post, w_in, ln_g, ln_b, w_s, bias_full, w_out):
    tm = ROW_TILE
    n = x2d.shape[0] // tm
    tile = pl.BlockSpec((tm, D_MODEL), lambda i: (i, 0))
    return pl.pallas_call(
        _gmlp_kernel,
        grid=(n,),
        in_specs=[
            tile,
            _resident((None, 1, D_MODEL), _layer_of(i, 3)),
            _resident((None, 1, D_MODEL), _layer_of(i, 3)),
            _resident((None, D_MODEL, 3 * BRANCH), _layer_of(j, 3)),
            _resident((None, 1, BRANCH), _layer_of(j, 3)),
            _resident((None, 1, BRANCH), _layer_of(j, 3)),
            _resident((None, A_GROUPS, GMLP_BLOCK, GMLP_BLOCK), _layer_of(j, 4)),
            _resident((GMLP_BLOCK, BRANCH), lambda i: (0, 0)),
            _resident((None, BRANCH, D_MODEL), _layer_of(j, 3)),
        ],
        out_specs=tile,
        out_shape=jax.ShapeDtypeStruct(x2d.shape, x2d.dtype),
        scratch_shapes=[
            pltpu.VMEM((tm, D_MODEL), jnp.bfloat16),
            pltpu.VMEM((tm, BRANCH), jnp.float32),
            pltpu.VMEM((tm, BRANCH), jnp.float32),
            pltpu.VMEM((tm, BRANCH), jnp.bfloat16),
            pltpu.VMEM((D_MODEL, 3 * BRANCH), jnp.bfloat16),
            pltpu.VMEM((BRANCH, D_MODEL), jnp.bfloat16),
        ],
        compiler_params=_COMPILER_PARAMS,
        name="gmlp_layer",
    )(x2d, norm_pre, norm_post, w_in, ln_g, ln_b, w_s, bias_full, w_out)


def _pool_kernel(tiles_per_seq, x_ref, gpre_ref, gpost_ref, win_ref, wgrp_ref,
                 scale_ref, wout_ref, o_ref, h_s, carry_s, y_s, win_s, wgrp_s, wout_s):
    tm = x_ref.shape[0]
    s = pl.program_id(0)

    @pl.when(s == 0)
    def _():
        _copy_rows(win_ref, win_s)
        for g in range(B_GROUPS):
            wgrp_s[g] = wgrp_ref[g]
        _copy_rows(wout_ref, wout_s)
        carry_s[...] = jnp.zeros_like(carry_s)

    x = x_ref[...]
    h_s[...] = (x * _rms_scale(x) * gpre_ref[...]).astype(jnp.bfloat16)

    sidx = s % tiles_per_seq
    pos1 = sidx * tm + lax.broadcasted_iota(jnp.int32, (tm, 1), 0) + 1
    keep = (sidx != 0).astype(jnp.float32)

    for g, win in enumerate(POOL_WINDOWS):
        c0 = g * B_GROUP_DIM
        cols = slice(c0, c0 + B_GROUP_DIM)
        xb = _dot(h_s[...], win_s[:, c0:c0 + B_GROUP_DIM])
        z = _dot(h_s[...], win_s[:, BRANCH + c0:BRANCH + c0 + B_GROUP_DIM])
        ext = jnp.concatenate([carry_s[:, cols] * keep, xb], axis=0)
        carry_s[:, cols] = xb[tm - HALO:, :]
        acc = ext
        k = 1
        while k < win:
            acc = acc + pltpu.roll(acc, k, 0)
            k *= 2
        wsum = acc[HALO:, :]
        inv_count = 1.0 / jnp.minimum(pos1, win).astype(jnp.float32)
        pooled = (wsum * inv_count - xb).astype(jnp.bfloat16)
        mixed = _dot(pooled, wgrp_s[g]) * scale_ref[:, cols]
        y_s[:, cols] = (mixed * _silu(z)).astype(jnp.bfloat16)

    out = _dot(y_s[...], wout_s[...])
    o_ref[...] = x_ref[...] + out * _rms_scale(out) * gpost_ref[...]


def _pool_layer(x2d, seq, i, j, norm_pre, norm_post, w_in, w_grp, scale, w_out):
    tm = ROW_TILE
    n = x2d.shape[0] // tm
    tile = pl.BlockSpec((tm, D_MODEL), lambda i: (i, 0))
    return pl.pallas_call(
        functools.partial(_pool_kernel, seq // tm),
        grid=(n,),
        in_specs=[
            tile,
            _resident((None, 1, D_MODEL), _layer_of(i, 3)),
            _resident((None, 1, D_MODEL), _layer_of(i, 3)),
            _resident((None, D_MODEL, 2 * BRANCH), _layer_of(j, 3)),
            _resident((None, B_GROUPS, B_GROUP_DIM, B_GROUP_DIM), _layer_of(j, 4)),
            _resident((None, 1, BRANCH), _layer_of(j, 3)),
            _resident((None, BRANCH, D_MODEL), _layer_of(j, 3)),
        ],
        out_specs=tile,
        out_shape=jax.ShapeDtypeStruct(x2d.shape, x2d.dtype),
        scratch_shapes=[
            pltpu.VMEM((tm, D_MODEL), jnp.bfloat16),
            pltpu.VMEM((HALO, BRANCH), jnp.float32),
            pltpu.VMEM((tm, BRANCH), jnp.bfloat16),
            pltpu.VMEM((D_MODEL, 2 * BRANCH), jnp.bfloat16),
            pltpu.VMEM((B_GROUPS, B_GROUP_DIM, B_GROUP_DIM), jnp.bfloat16),
            pltpu.VMEM((BRANCH, D_MODEL), jnp.bfloat16),
        ],
        compiler_params=_COMPILER_PARAMS,
        name="pool_layer",
    )(x2d, norm_pre, norm_post, w_in, w_grp, scale, w_out)


def kernel(x, norm_pre, norm_post, a_w_in, a_ln_g, a_ln_b, a_w_s, a_b_s, a_w_out,
           b_w_in, b_w_grp, b_scale, b_w_out):
    bsz, seq, d = x.shape
    assert d == D_MODEL and seq % ROW_TILE == 0 and ROW_TILE % GMLP_BLOCK == 0 and ROW_TILE % HALO == 0
    depth = norm_pre.shape[0]
    bf16 = jnp.bfloat16
    rows = lambda p: p[:, None, :]
    norm_pre, norm_post = rows(norm_pre), rows(norm_post)
    a_w_in, a_w_out = a_w_in.astype(bf16), a_w_out.astype(bf16)
    b_w_in, b_w_grp, b_w_out = b_w_in.astype(bf16), b_w_grp.astype(bf16), b_w_out.astype(bf16)
    a_ln_g, a_ln_b, b_scale = rows(a_ln_g), rows(a_ln_b), rows(b_scale)
    x = x.reshape(bsz * seq, d)
    for i in range(depth):
        j = i // 2
        if i % 2 == 0:
            bias_full = jnp.repeat(jnp.transpose(a_b_s[j]), A_GROUP_DIM, axis=1)
            x = _gmlp_layer(x, i, j, norm_pre, norm_post, a_w_in, a_ln_g, a_ln_b, a_w_s,
                            bias_full, a_w_out)
        else:
            x = _pool_layer(x, seq, i, j, norm_pre, norm_post, b_w_in, b_w_grp, b_scale, b_w_out)
    return x.reshape(bsz, seq, d)
```

```python
import functools

import jax
import jax.numpy as jnp
from jax import lax
from jax.experimental import pallas as pl
from jax.experimental.pallas import tpu as pltpu

D_MODEL = 1024
BRANCH = 2048
CHUNK = 64
GMLP_BLOCK = 128
A_GROUPS = 8
A_GROUP_DIM = BRANCH // A_GROUPS
POOL_WINDOWS = (2, 4, 8, 16)
B_GROUPS = len(POOL_WINDOWS)
B_GROUP_DIM = BRANCH // B_GROUPS
HALO = max(POOL_WINDOWS)
EPS = 1e-6

ROW_TILE = 256
PACK_ROWS = 128
VMEM_LIMIT_BYTES = 56 * 1024 * 1024

_GELU_C = 0.7978845608028654


def _gelu_tanh(x):
    return 0.5 * x * (1.0 + jnp.tanh(_GELU_C * (x + 0.044715 * (x * x * x))))


def _silu(x):
    return x * jax.nn.sigmoid(x)


def _rms_scale(x):
    return lax.rsqrt(jnp.mean(x * x, axis=-1, keepdims=True) + EPS)


def _dot(a, b):
    return jnp.dot(a, b, preferred_element_type=jnp.float32)


def _pack_kernel(w_ref, o_ref):
    o_ref[...] = pltpu.bitcast(w_ref[...].astype(jnp.bfloat16), jnp.uint32)


def _pack_rows(w):
    k, n = w.shape[-2:]
    w3 = w.reshape((-1, k, n))
    return pl.pallas_call(
        _pack_kernel,
        grid=(w3.shape[0], k // PACK_ROWS),
        in_specs=[pl.BlockSpec((None, PACK_ROWS, n), lambda l, r: (l, r, 0))],
        out_specs=pl.BlockSpec((None, PACK_ROWS // 2, n), lambda l, r: (l, r, 0)),
        out_shape=jax.ShapeDtypeStruct((w3.shape[0], k // 2, n), jnp.uint32),
        name="pack_weights",
    )(w3).reshape(w.shape[:-2] + (k // 2, n))


def _unpack_rows(w32):
    return pltpu.bitcast(w32, jnp.bfloat16)


def _gmlp_kernel(x_ref, gpre_ref, gpost_ref, win_ref, lng_ref, lnb_ref, ws_ref,
                 bias_ref, wout_ref, o_ref, h_s, uz_s, gv_s, y_s):
    tm = x_ref.shape[0]
    x = x_ref[...]
    h_s[...] = (x * _rms_scale(x) * gpre_ref[...]).astype(jnp.bfloat16)

    gv = _gelu_tanh(_dot(h_s[...], _unpack_rows(win_ref[:, BRANCH:2 * BRANCH])))
    mu = jnp.mean(gv, axis=-1, keepdims=True)
    var = jnp.mean(gv * gv, axis=-1, keepdims=True) - mu * mu
    rstd = lax.rsqrt(var + EPS)
    gv_s[...] = gv

    u = _dot(h_s[...], _unpack_rows(win_ref[:, 0:BRANCH]))
    z = _dot(h_s[...], _unpack_rows(win_ref[:, 2 * BRANCH:3 * BRANCH]))
    uz_s[...] = _gelu_tanh(u) * _silu(z)

    p_chunk = lax.broadcasted_iota(jnp.int32, (GMLP_BLOCK, GMLP_BLOCK), 0) // CHUNK
    q_chunk = lax.broadcasted_iota(jnp.int32, (GMLP_BLOCK, GMLP_BLOCK), 1) // CHUNK
    causal = q_chunk <= p_chunk

    for g in range(A_GROUPS):
        c0 = g * A_GROUP_DIM
        cols = slice(c0, c0 + A_GROUP_DIM)
        w_g = jnp.where(causal, ws_ref[g], 0.0).astype(jnp.bfloat16)
        vn = ((gv_s[:, cols] - mu) * rstd * lng_ref[:, cols] + lnb_ref[:, cols]).astype(jnp.bfloat16)
        for b in range(tm // GMLP_BLOCK):
            rows = slice(b * GMLP_BLOCK, (b + 1) * GMLP_BLOCK)
            mixed = _dot(w_g, vn[rows]) + bias_ref[:, cols]
            y_s[rows, cols] = (uz_s[rows, cols] * mixed).astype(jnp.bfloat16)

    out = _dot(y_s[...], _unpack_rows(wout_ref[...]))
    o_ref[...] = x_ref[...] + out * _rms_scale(out) * gpost_ref[...]


def _resident(block_shape, index_map):
    return pl.BlockSpec(block_shape, index_map, pipeline_mode=pl.Buffered(1))


def _layer_of(j, ndim):
    return lambda i: (j,) + (0,) * (ndim - 1)


_COMPILER_PARAMS = pltpu.CompilerParams(
    dimension_semantics=("arbitrary",),
    vmem_limit_bytes=VMEM_LIMIT_BYTES)


def _gmlp_layer(x2d, i, j, norm_pre, norm_post, w_in, ln_g, ln_b, w_s, bias_full, w_out):
    tm = ROW_TILE
    n = x2d.shape[0] // tm
    tile = pl.BlockSpec((tm, D_MODEL), lambda i: (i, 0))
    return pl.pallas_call(
        _gmlp_kernel,
        grid=(n,),
        in_specs=[
            tile,
            _resident((None, 1, D_MODEL), _layer_of(i, 3)),
            _resident((None, 1, D_MODEL), _layer_of(i, 3)),
            _resident((None, D_MODEL // 2, 3 * BRANCH), _layer_of(j, 3)),
            _resident((None, 1, BRANCH), _layer_of(j, 3)),
            _resident((None, 1, BRANCH), _layer_of(j, 3)),
            _resident((None, A_GROUPS, GMLP_BLOCK, GMLP_BLOCK), _layer_of(j, 4)),
            _resident((GMLP_BLOCK, BRANCH), lambda i: (0, 0)),
            _resident((None, BRANCH // 2, D_MODEL), _layer_of(j, 3)),
        ],
        out_specs=tile,
        out_shape=jax.ShapeDtypeStruct(x2d.shape, x2d.dtype),
        scratch_shapes=[
            pltpu.VMEM((tm, D_MODEL), jnp.bfloat16),
            pltpu.VMEM((tm, BRANCH), jnp.float32),
            pltpu.VMEM((tm, BRANCH), jnp.float32),
            pltpu.VMEM((tm, BRANCH), jnp.bfloat16),
        ],
        compiler_params=_COMPILER_PARAMS,
        name="gmlp_layer",
    )(x2d, norm_pre, norm_post, w_in, ln_g, ln_b, w_s, bias_full, w_out)


def _pool_kernel(tiles_per_seq, x_ref, gpre_ref, gpost_ref, win_ref, wgrp_ref,
                 scale_ref, wout_ref, o_ref, h_s, carry_s, y_s):
    tm = x_ref.shape[0]
    s = pl.program_id(0)

    @pl.when(s == 0)
    def _():
        carry_s[...] = jnp.zeros_like(carry_s)

    x = x_ref[...]
    h_s[...] = (x * _rms_scale(x) * gpre_ref[...]).astype(jnp.bfloat16)

    sidx = s % tiles_per_seq
    pos1 = sidx * tm + lax.broadcasted_iota(jnp.int32, (tm, 1), 0) + 1
    keep = (sidx != 0).astype(jnp.float32)

    for g, win in enumerate(POOL_WINDOWS):
        c0 = g * B_GROUP_DIM
        cols = slice(c0, c0 + B_GROUP_DIM)
        xb = _dot(h_s[...], _unpack_rows(win_ref[:, c0:c0 + B_GROUP_DIM]))
        z = _dot(h_s[...], _unpack_rows(win_ref[:, BRANCH + c0:BRANCH + c0 + B_GROUP_DIM]))
        ext = jnp.concatenate([carry_s[:, cols] * keep, xb], axis=0)
        carry_s[:, cols] = xb[tm - HALO:, :]
        acc = ext
        k = 1
        while k < win:
            acc = acc + pltpu.roll(acc, k, 0)
            k *= 2
        wsum = acc[HALO:, :]
        inv_count = 1.0 / jnp.minimum(pos1, win).astype(jnp.float32)
        pooled = (wsum * inv_count - xb).astype(jnp.bfloat16)
        mixed = _dot(pooled, _unpack_rows(wgrp_ref[g])) * scale_ref[:, cols]
        y_s[:, cols] = (mixed * _silu(z)).astype(jnp.bfloat16)

    out = _dot(y_s[...], _unpack_rows(wout_ref[...]))
    o_ref[...] = x_ref[...] + out * _rms_scale(out) * gpost_ref[...]


def _pool_layer(x2d, seq, i, j, norm_pre, norm_post, w_in, w_grp, scale, w_out):
    tm = ROW_TILE
    n = x2d.shape[0] // tm
    tile = pl.BlockSpec((tm, D_MODEL), lambda i: (i, 0))
    return pl.pallas_call(
        functools.partial(_pool_kernel, seq // tm),
        grid=(n,),
        in_specs=[
            tile,
            _resident((None, 1, D_MODEL), _layer_of(i, 3)),
            _resident((None, 1, D_MODEL), _layer_of(i, 3)),
            _resident((None, D_MODEL // 2, 2 * BRANCH), _layer_of(j, 3)),
            _resident((None, B_GROUPS, B_GROUP_DIM // 2, B_GROUP_DIM), _layer_of(j, 4)),
            _resident((None, 1, BRANCH), _layer_of(j, 3)),
            _resident((None, BRANCH // 2, D_MODEL), _layer_of(j, 3)),
        ],
        out_specs=tile,
        out_shape=jax.ShapeDtypeStruct(x2d.shape, x2d.dtype),
        scratch_shapes=[
            pltpu.VMEM((tm, D_MODEL), jnp.bfloat16),
            pltpu.VMEM((HALO, BRANCH), jnp.float32),
            pltpu.VMEM((tm, BRANCH), jnp.bfloat16),
        ],
        compiler_params=_COMPILER_PARAMS,
        name="pool_layer",
    )(x2d, norm_pre, norm_post, w_in, w_grp, scale, w_out)


def kernel(x, norm_pre, norm_post, a_w_in, a_ln_g, a_ln_b, a_w_s, a_b_s, a_w_out,
           b_w_in, b_w_grp, b_scale, b_w_out):
    bsz, seq, d = x.shape
    assert d == D_MODEL and seq % ROW_TILE == 0 and ROW_TILE % GMLP_BLOCK == 0 and ROW_TILE % HALO == 0
    depth = norm_pre.shape[0]
    rows = lambda p: p[:, None, :]
    norm_pre, norm_post = rows(norm_pre), rows(norm_post)
    a_w_in, a_w_out = _pack_rows(a_w_in), _pack_rows(a_w_out)
    b_w_in, b_w_grp, b_w_out = _pack_rows(b_w_in), _pack_rows(b_w_grp), _pack_rows(b_w_out)
    a_ln_g, a_ln_b, b_scale = rows(a_ln_g), rows(a_ln_b), rows(b_scale)
    x = x.reshape(bsz * seq, d)
    for i in range(depth):
        j = i // 2
        if i % 2 == 0:
            bias_full = jnp.repeat(jnp.transpose(a_b_s[j]), A_GROUP_DIM, axis=1)
            x = _gmlp_layer(x, i, j, norm_pre, norm_post, a_w_in, a_ln_g, a_ln_b, a_w_s,
                            bias_full, a_w_out)
        else:
            x = _pool_layer(x, seq, i, j, norm_pre, norm_post, b_w_in, b_w_grp, b_scale, b_w_out)
    return x.reshape(bsz, seq, d)
```
